```python
import math
import jax, jax.numpy as jnp
from jax import lax
import numpy as np

D_MODEL = 2048
BATCH = 2
SEQ = 4096
DEPTH = 1

CHUNK = 64
D_MIX = D_MODEL
D_CONV = D_MIX // 2
D_SSM = D_MIX - D_CONV
CONV_WIDTH = 31
SSM_GROUP = 16
SSM_GROUPS = D_SSM // SSM_GROUP
SSM_STATE = 64
N_MEM = 256
MEM_HEADS = 4
MEM_HEAD_DIM = D_MODEL // MEM_HEADS
PEER_HEADS = 8
PEER_KEYS = 128
PEER_EXPERTS = PEER_KEYS * PEER_KEYS
PEER_DKEY = 128
PEER_HALF = PEER_DKEY // 2
PEER_TOPK = 16
PEER_BLOCK = 128
EPS = 1e-6

kernel_name = "hybrid_conv_s5_peer_memxattn"

F32 = jnp.float32


def rms_norm(x, g):
    xf = x.astype(F32)
    y = xf * lax.rsqrt(jnp.mean(xf * xf, axis=-1, keepdims=True) + EPS)
    return (y * g.astype(F32)).astype(x.dtype)


def layer_norm(x, g, b):
    xf = x.astype(F32)
    mu = jnp.mean(xf, axis=-1, keepdims=True)
    xc = xf - mu
    y = xc * lax.rsqrt(jnp.mean(xc * xc, axis=-1, keepdims=True) + EPS)
    return (y * g.astype(F32) + b.astype(F32)).astype(x.dtype)


def causal_depthwise_conv(x, w, b):
    k = w.shape[0]
    y = lax.conv_general_dilated(
        x, w[:, None, :].astype(x.dtype), window_strides=(1,), padding=[(k - 1, 0)],
        dimension_numbers=("NWC", "WIO", "NWC"), feature_group_count=x.shape[-1])
    return y + b.astype(x.dtype)


def conformer_conv_group(a, gate, conv_w, conv_b, ln_g, ln_b):
    c = a * jax.nn.sigmoid(gate)
    c = causal_depthwise_conv(c, conv_w, conv_b)
    c = layer_norm(c, ln_g, ln_b)
    return c * jax.nn.sigmoid(c)


def s5_group(u, lam_re, lam_im, log_dt, b_re, b_im, c_re, c_im, d, glu_w):
    bsz, length, _ = u.shape
    uf = u.astype(F32).reshape(bsz, length, SSM_GROUPS, SSM_GROUP)
    lam = lax.complex(lam_re.astype(F32), lam_im.astype(F32))
    dt = jnp.exp(log_dt.astype(F32))[:, None]
    lam_bar = jnp.exp(lam * dt)
    b = lax.complex(b_re.astype(F32), b_im.astype(F32))
    b_bar = ((lam_bar - 1.0) / lam)[..., None] * b
    bu = jnp.einsum("blgh,gph->blgp", uf.astype(jnp.complex64), b_bar)
    a = jnp.broadcast_to(lam_bar, bu.shape)

    def combine(left, right):
        a_l, s_l = left
        a_r, s_r = right
        return a_r * a_l, a_r * s_l + s_r

    _, states = lax.associative_scan(combine, (a, bu), axis=1)
    c = lax.complex(c_re.astype(F32), c_im.astype(F32))
    y = jnp.real(jnp.einsum("blgp,ghp->blgh", states, c))
    y = y + d.astype(F32).reshape(SSM_GROUPS, SSM_GROUP) * uf
    y = y.reshape(bsz, length, D_SSM).astype(u.dtype)
    z = jax.nn.gelu(y)
    return z * jax.nn.sigmoid(z @ glu_w)


def mem_cross_attention(h, mem_n, w_q, w_kv, w_o):
    bsz, length, _ = h.shape
    q = (h @ w_q).reshape(bsz, length, MEM_HEADS, MEM_HEAD_DIM)
    kv = (mem_n @ w_kv).reshape(bsz, mem_n.shape[1], 2, MEM_HEADS, MEM_HEAD_DIM)
    k, v = kv[:, :, 0], kv[:, :, 1]
    s = jnp.einsum("blhd,bmhd->bhlm", q, k).astype(F32) * (1.0 / math.sqrt(MEM_HEAD_DIM))
    p = jax.nn.softmax(s, axis=-1).astype(h.dtype)
    o = jnp.einsum("bhlm,bmhd->blhd", p, v).reshape(bsz, length, D_MODEL)
    return o @ w_o


def peer(h, w_q, sub_keys, u_tab, v_tab):
    bsz, length, dm = h.shape
    q = (h @ w_q).astype(F32).reshape(bsz, length, PEER_HEADS, 2, PEER_HALF)
    s = jnp.einsum("blhsk,hsnk->blhsn", q, sub_keys.astype(F32))
    s1, i1 = lax.top_k(s[..., 0, :], PEER_TOPK)
    s2, i2 = lax.top_k(s[..., 1, :], PEER_TOPK)
    cand_s = (s1[..., :, None] + s2[..., None, :]).reshape(bsz, length, PEER_HEADS, PEER_TOPK * PEER_TOPK)
    cand_i = (i1[..., :, None] * PEER_KEYS + i2[..., None, :]).reshape(bsz, length, PEER_HEADS, PEER_TOPK * PEER_TOPK)
    top_s, pos = lax.top_k(cand_s, PEER_TOPK)
    idx = jnp.take_along_axis(cand_i, pos, axis=-1)
    g = jax.nn.softmax(top_s, axis=-1)
    n_tok = bsz * length
    nb = n_tok // PEER_BLOCK
    xb = h.reshape(nb, PEER_BLOCK, dm)
    ib = idx.reshape(nb, PEER_BLOCK, PEER_HEADS * PEER_TOPK)
    gb = g.astype(h.dtype).reshape(nb, PEER_BLOCK, PEER_HEADS * PEER_TOPK)

    def block(args):
        xt, it, gt = args
        u = jnp.take(u_tab, it, axis=0)
        act = jax.nn.gelu(jnp.einsum("td,tkd->tk", xt, u))
        v = jnp.take(v_tab, it, axis=0)
        return jnp.einsum("tk,tkd->td", gt * act, v)

    out = lax.map(block, (xb, ib, gb))
    return out.reshape(bsz, length, dm)


def setup_inputs(seed: int = 0) -> dict:
    key = jax.random.key(seed)
    ks = iter(jax.random.split(key, 40))
    nrm = lambda shape, scale: jax.random.normal(next(ks), shape, F32) * scale
    gain = lambda shape: 1.0 + 0.05 * jax.random.normal(next(ks), shape, F32)
    L = DEPTH
    n_in = 2 * D_CONV + D_SSM
    lam_re = -0.5 + 0.01 * jax.random.normal(next(ks), (L, SSM_GROUPS, SSM_STATE), F32)
    lam_im = (math.pi * jnp.arange(SSM_STATE, dtype=F32))[None, None, :] + 0.01 * jax.random.normal(next(ks), (L, SSM_GROUPS, SSM_STATE), F32)
    log_dt = jax.random.uniform(next(ks), (L, SSM_GROUPS), F32, math.log(1e-3), math.log(1e-1))
    return {
        "x": nrm((BATCH, SEQ, D_MODEL), 1.0),
        "mem": nrm((BATCH, N_MEM, D_MODEL), 1.0),
        "norm_mix_g": gain((L, D_MODEL)),
        "w_in": nrm((L, D_MODEL, n_in), D_MODEL ** -0.5),
        "conv_w": nrm((L, CONV_WIDTH, D_CONV), CONV_WIDTH ** -0.5),
        "conv_b": nrm((L, D_CONV), 0.02),
        "conv_ln_g": gain((L, D_CONV)),
        "conv_ln_b": nrm((L, D_CONV), 0.02),
        "ssm_lam_re": lam_re,
        "ssm_lam_im": lam_im,
        "ssm_log_dt": log_dt,
        "ssm_b_re": nrm((L, SSM_GROUPS, SSM_STATE, SSM_GROUP), (2.0 * SSM_GROUP) ** -0.5),
        "ssm_b_im": nrm((L, SSM_GROUPS, SSM_STATE, SSM_GROUP), (2.0 * SSM_GROUP) ** -0.5),
        "ssm_c_re": nrm((L, SSM_GROUPS, SSM_GROUP, SSM_STATE), (2.0 * SSM_STATE) ** -0.5),
        "ssm_c_im": nrm((L, SSM_GROUPS, SSM_GROUP, SSM_STATE), (2.0 * SSM_STATE) ** -0.5),
        "ssm_d": nrm((L, D_SSM), 1.0),
        "ssm_glu_w": nrm((L, D_SSM, D_SSM), D_SSM ** -0.5),
        "grp_norm_conv_g": gain((L, D_CONV)),
        "grp_norm_ssm_g": gain((L, D_SSM)),
        "w_out": nrm((L, D_MIX, D_MODEL), D_MIX ** -0.5),
        "norm_mem_g": gain((L, D_MODEL)),
        "mem_norm_g": gain((L, D_MODEL)),
        "w_q_mem": nrm((L, D_MODEL, D_MODEL), D_MODEL ** -0.5),
        "w_kv_mem": nrm((L, D_MODEL, 2 * D_MODEL), D_MODEL ** -0.5),
        "w_o_mem": nrm((L, D_MODEL, D_MODEL), D_MODEL ** -0.5),
        "norm_ffn_g": gain((L, D_MODEL)),
        "peer_w_q": nrm((L, D_MODEL, PEER_HEADS * PEER_DKEY), D_MODEL ** -0.5),
        "peer_keys": nrm((L, PEER_HEADS, 2, PEER_KEYS, PEER_HALF), PEER_HALF ** -0.5),
        "peer_u": nrm((L, PEER_EXPERTS, D_MODEL), D_MODEL ** -0.5),
        "peer_v": nrm((L, PEER_EXPERTS, D_MODEL), PEER_TOPK ** -0.5),
        "final_norm_g": gain((D_MODEL,)),
    }


def reference(x, mem, norm_mix_g, w_in, conv_w, conv_b, conv_ln_g, conv_ln_b,
              ssm_lam_re, ssm_lam_im, ssm_log_dt, ssm_b_re, ssm_b_im, ssm_c_re, ssm_c_im,
              ssm_d, ssm_glu_w, grp_norm_conv_g, grp_norm_ssm_g, w_out,
              norm_mem_g, mem_norm_g, w_q_mem, w_kv_mem, w_o_mem,
              norm_ffn_g, peer_w_q, peer_keys, peer_u, peer_v, final_norm_g):
    h = x
    for i in range(DEPTH):
        hn = rms_norm(h, norm_mix_g[i])
        proj = hn @ w_in[i]
        conv_a = proj[..., :D_CONV]
        conv_gate = proj[..., D_CONV:2 * D_CONV]
        ssm_u = proj[..., 2 * D_CONV:]
        yc = conformer_conv_group(conv_a, conv_gate, conv_w[i], conv_b[i], conv_ln_g[i], conv_ln_b[i])
        ys = s5_group(ssm_u, ssm_lam_re[i], ssm_lam_im[i], ssm_log_dt[i], ssm_b_re[i], ssm_b_im[i],
                      ssm_c_re[i], ssm_c_im[i], ssm_d[i], ssm_glu_w[i])
        y = jnp.concatenate([rms_norm(yc, grp_norm_conv_g[i]), rms_norm(ys, grp_norm_ssm_g[i])], axis=-1)
        h = h + y @ w_out[i]
        hn = rms_norm(h, norm_mem_g[i])
        mem_n = rms_norm(mem, mem_norm_g[i])
        h = h + mem_cross_attention(hn, mem_n, w_q_mem[i], w_kv_mem[i], w_o_mem[i])
        hn = rms_norm(h, norm_ffn_g[i])
        h = h + peer(hn, peer_w_q[i], peer_keys[i], peer_u[i], peer_v[i])
    return rms_norm(h, final_norm_g)
```

```python
import functools
import math

import jax
import jax.numpy as jnp
from jax import lax
from jax.experimental import pallas as pl
from jax.experimental.pallas import tpu as pltpu

F32 = jnp.float32
BF16 = jnp.bfloat16
EPS = 1e-6
MEM_HEADS = 4
PEER_TOPK = 16
LANES = 128
CONV_HALO = 32
S5_CHUNK = 64
VMEM_LIMIT = 56 * 1024 * 1024

_NT = (((1,), (1,)), ((), ()))
_TN = (((0,), (0,)), ((), ()))


def _rms(x, g):
    return x * lax.rsqrt(jnp.mean(x * x, axis=-1, keepdims=True) + EPS) * g


def _dot(a, b):
    return jnp.dot(a, b, preferred_element_type=F32)


def _const_spec(shape):
    nd = len(shape)
    return pl.BlockSpec(shape, lambda *_: (0,) * nd, pipeline_mode=pl.Buffered(1))


def _params(sem):
    return pltpu.CompilerParams(dimension_semantics=sem, vmem_limit_bytes=VMEM_LIMIT)


def _mix_in_kernel(x_ref, g_ref, w_ref, cw_ref, cb_ref, lng_ref, lnb_ref, gng_ref,
                   yc_ref, u_ref, xe_ref, *, blocks_per_seq, dc, kw, rb):
    i = pl.program_id(0)
    tm = x_ref.shape[0]
    hn = _rms(x_ref[...], g_ref[...]).astype(BF16)
    a = _dot(hn, w_ref[:, 0:dc])
    gate = _dot(hn, w_ref[:, dc:2 * dc])
    u_ref[...] = _dot(hn, w_ref[:, 2 * dc:])

    @pl.when(i % blocks_per_seq == 0)
    def _():
        xe_ref[0:CONV_HALO, :] = jnp.zeros((CONV_HALO, dc), F32)

    xe_ref[CONV_HALO:CONV_HALO + tm, :] = a * jax.nn.sigmoid(gate)

    for r0 in range(0, tm, rb):
        acc = jnp.broadcast_to(cb_ref[...], (rb, dc))
        for k in range(kw):
            start = r0 + CONV_HALO - (kw - 1) + k
            acc = acc + cw_ref[k:k + 1, :] * xe_ref[start:start + rb, :]
        mu = jnp.mean(acc, axis=-1, keepdims=True)
        xc = acc - mu
        y = xc * lax.rsqrt(jnp.mean(xc * xc, axis=-1, keepdims=True) + EPS) * lng_ref[...] + lnb_ref[...]
        y = y * jax.nn.sigmoid(y)
        yc_ref[r0:r0 + rb, :] = _rms(y, gng_ref[...]).astype(BF16)
    xe_ref[0:CONV_HALO, :] = xe_ref[tm:tm + CONV_HALO, :]


def _mix_in(x2, g, w_in, conv_w, conv_b, ln_g, ln_b, gn_g, *, seq_len, tm):
    n, d = x2.shape
    kw, dc = conv_w.shape
    ds = w_in.shape[1] - 2 * dc
    assert n % tm == 0 and seq_len % tm == 0 and kw - 1 <= CONV_HALO <= tm
    kern = functools.partial(_mix_in_kernel, blocks_per_seq=seq_len // tm, dc=dc, kw=kw, rb=32)
    return pl.pallas_call(
        kern,
        grid=(n // tm,),
        in_specs=[
            pl.BlockSpec((tm, d), lambda i: (i, 0)),
            _const_spec((1, d)),
            _const_spec(w_in.shape),
            _const_spec((kw, dc)),
            _const_spec((1, dc)),
            _const_spec((1, dc)),
            _const_spec((1, dc)),
            _const_spec((1, dc)),
        ],
        out_specs=[
            pl.BlockSpec((tm, dc), lambda i: (i, 0)),
            pl.BlockSpec((tm, ds), lambda i: (i, 0)),
        ],
        out_shape=[jax.ShapeDtypeStruct((n, dc), BF16), jax.ShapeDtypeStruct((n, ds), F32)],
        scratch_shapes=[pltpu.VMEM((tm + CONV_HALO, dc), F32)],
        compiler_params=_params(("arbitrary",)),
        name="mix_in",
    )(x2, g, w_in, conv_w, conv_b, ln_g, ln_b, gn_g)


def _rep_rows(a, reps):
    t, p = a.shape
    return jnp.concatenate([jnp.broadcast_to(a[k:k + 1, :], (reps, p)) for k in range(t)], axis=0)


def _tile_rows(a, reps):
    h, p = a.shape
    return jnp.broadcast_to(a[None], (reps, h, p)).reshape(reps * h, p)


def _s5_kernel(uc_ref, lre_ref, lim_ref, ldt_ref, btr_ref, bti_ref, cr_ref, ci_ref, y_ref,
               m_ref, slr_ref, sli_ref, spr_ref, spi_ref, *, t_chunk, chunks_per_seq, nseq):
    h, p = cr_ref.shape[1], cr_ref.shape[2]
    w = t_chunk * h
    lre, lim = lre_ref[0], lim_ref[0]
    dt = jnp.exp(ldt_ref[0])
    are, aim = lre * dt, lim * dt
    mag = jnp.exp(are)
    lbr, lbi = mag * jnp.cos(aim), mag * jnp.sin(aim)
    den = lre * lre + lim * lim
    kr = ((lbr - 1.0) * lre + lbi * lim) / den
    ki = (lbi * lre - (lbr - 1.0) * lim) / den
    btr, bti = btr_ref[0], bti_ref[0]
    bbr, bbi = kr * btr - ki * bti, kr * bti + ki * btr
    cr, ci = cr_ref[0], ci_ref[0]

    tau = lax.broadcasted_iota(jnp.int32, (t_chunk, 1), 0).astype(F32)

    def lam_pow(e):
        m = jnp.exp(e * are)
        return m * jnp.cos(e * aim), m * jnp.sin(e * aim)

    l0r, l0i = lam_pow(tau)
    lrr, lri = lam_pow((t_chunk - 1.0) - tau)
    ltr, lti = lam_pow(jnp.full((1, 1), float(t_chunk), F32))

    l0r_rep, l0i_rep = _rep_rows(l0r, h), _rep_rows(l0i, h)
    c_r, c_i = _tile_rows(cr, t_chunk), _tile_rows(ci, t_chunk)
    lc0r = l0r_rep * c_r - l0i_rep * c_i
    lc0i = l0r_rep * c_i + l0i_rep * c_r
    lc1r = lc0r * lbr - lc0i * lbi
    lc1i = lc0r * lbi + lc0i * lbr

    r0 = (lax.dot_general(bbr, lc0r, _NT, preferred_element_type=F32, precision=lax.Precision.HIGHEST)
          - lax.dot_general(bbi, lc0i, _NT, preferred_element_type=F32, precision=lax.Precision.HIGHEST))
    ext = jnp.concatenate([jnp.zeros((h, w), F32), r0], axis=1)
    per_tile = LANES // h
    for b in range(per_tile):
        sb = ext[:, LANES - h * b: LANES - h * b + 2 * w - LANES]
        for a in range(w // LANES):
            i = per_tile * a + b
            m_ref[h * i:h * (i + 1), :] = sb[:, w - LANES * (a + 1): 2 * w - LANES * (a + 1)].astype(BF16)

    lrr_rep, lri_rep = _rep_rows(lrr, h), _rep_rows(lri, h)
    b_r, b_i = _tile_rows(bbr, t_chunk), _tile_rows(bbi, t_chunk)
    wsr = (lrr_rep * b_r - lri_rep * b_i).astype(BF16)
    wsi = (lrr_rep * b_i + lri_rep * b_r).astype(BF16)
    uc = uc_ref[0]
    slr_ref[...] = _dot(uc, wsr)
    sli_ref[...] = _dot(uc, wsi)

    def step(c, carry):
        new = []
        for s in range(nseq):
            sr, si = carry[2 * s], carry[2 * s + 1]
            r = s * chunks_per_seq + c
            spr_ref[pl.ds(r, 1), :] = sr
            spi_ref[pl.ds(r, 1), :] = si
            new.append(ltr * sr - lti * si + slr_ref[pl.ds(r, 1), :])
            new.append(ltr * si + lti * sr + sli_ref[pl.ds(r, 1), :])
        return tuple(new)

    lax.fori_loop(0, chunks_per_seq, step, tuple(jnp.zeros((1, p), F32) for _ in range(2 * nseq)))

    y = _dot(uc, m_ref[...])
    y = y + lax.dot_general(spr_ref[...].astype(BF16), lc1r.astype(BF16), _NT, preferred_element_type=F32)
    y = y - lax.dot_general(spi_ref[...].astype(BF16), lc1i.astype(BF16), _NT, preferred_element_type=F32)
    y_ref[0] = y


def _s5_core(uc, lam_re, lam_im, log_dt, bt_re, bt_im, c_re, c_im, *, t_chunk, chunks_per_seq, nseq):
    g, r, w = uc.shape
    h, p = c_re.shape[1], c_re.shape[2]
    assert w == t_chunk * h and LANES % h == 0 and w % LANES == 0 and r == nseq * chunks_per_seq
    kern = functools.partial(_s5_kernel, t_chunk=t_chunk, chunks_per_seq=chunks_per_seq, nseq=nseq)
    grp = lambda *shape: pl.BlockSpec((1,) + shape, lambda i: (i,) + (0,) * len(shape))
    return pl.pallas_call(
        kern,
        grid=(g,),
        in_specs=[grp(r, w), grp(1, p), grp(1, p), grp(1, 1), grp(h, p), grp(h, p), grp(h, p), grp(h, p)],
        out_specs=grp(r, w),
        out_shape=jax.ShapeDtypeStruct((g, r, w), F32),
        scratch_shapes=[pltpu.VMEM((w, w), BF16)] + [pltpu.VMEM((r, p), F32)] * 4,
        compiler_params=_params(("arbitrary",)),
        name="s5_core",
    )(uc, lam_re, lam_im, log_dt, bt_re, bt_im, c_re, c_im)


def _mix_out_kernel(ys_ref, u_ref, d_ref, gluw_ref, gsg_ref, yc_ref, wout_ref, x_ref, h_ref, *, dc):
    y = ys_ref[...] + d_ref[...] * u_ref[...]
    z = jax.nn.gelu(y)
    zz = z * jax.nn.sigmoid(_dot(z.astype(BF16), gluw_ref[...]))
    ysn = _rms(zz, gsg_ref[...]).astype(BF16)
    h_ref[...] = x_ref[...] + _dot(yc_ref[...], wout_ref[0:dc, :]) + _dot(ysn, wout_ref[dc:, :])


def _mix_out(ys, u, d, glu_w, gs_g, yc, w_out, x2, *, tm):
    n, dm = x2.shape
    dc, ds = yc.shape[1], ys.shape[1]
    tok = lambda c: pl.BlockSpec((tm, c), lambda i: (i, 0))
    return pl.pallas_call(
        functools.partial(_mix_out_kernel, dc=dc),
        grid=(n // tm,),
        in_specs=[tok(ds), tok(ds), _const_spec((1, ds)), _const_spec(glu_w.shape), _const_spec((1, ds)),
                  tok(dc), _const_spec(w_out.shape), tok(dm)],
        out_specs=tok(dm),
        out_shape=jax.ShapeDtypeStruct((n, dm), F32),
        compiler_params=_params(("arbitrary",)),
        name="mix_out",
    )(ys, u, d, glu_w, gs_g, yc, w_out, x2)


def _kv_kernel(mem_ref, g_ref, w_ref, kv_ref):
    kv_ref[...] = _dot(_rms(mem_ref[...], g_ref[...]).astype(BF16), w_ref[...]).astype(BF16)


def _kv_proj(mem2, g, w_kv, *, tn):
    nm, d = mem2.shape
    return pl.pallas_call(
        _kv_kernel,
        grid=(w_kv.shape[1] // tn,),
        in_specs=[_const_spec((nm, d)), _const_spec((1, d)), pl.BlockSpec((d, tn), lambda j: (0, j))],
        out_specs=pl.BlockSpec((nm, tn), lambda j: (0, j)),
        out_shape=jax.ShapeDtypeStruct((nm, w_kv.shape[1]), BF16),
        compiler_params=_params(("arbitrary",)),
        name="kv_proj",
    )(mem2, g, w_kv)


def _attn_kernel(h_ref, g_ref, wq_ref, k_ref, v_ref, wo_ref, o_ref, *, heads):
    hres = h_ref[...]
    d = hres.shape[1]
    hd = d // heads
    q = _dot(_rms(hres, g_ref[...]).astype(BF16), wq_ref[...])
    scale = 1.0 / math.sqrt(hd)
    outs = []
    for hh in range(heads):
        cs = slice(hh * hd, (hh + 1) * hd)
        s = lax.dot_general(q[:, cs].astype(BF16), k_ref[:, cs], _NT, preferred_element_type=F32) * scale
        e = jnp.exp(s - jnp.max(s, axis=-1, keepdims=True))
        pr = e / jnp.sum(e, axis=-1, keepdims=True)
        outs.append(_dot(pr.astype(BF16), v_ref[:, cs]).astype(BF16))
    o_ref[...] = hres + _dot(jnp.concatenate(outs, axis=1), wo_ref[...])


def _mem_attn(h, g, w_q, kv, w_o, *, seq_len, n_mem, tm):
    n, d = h.shape
    bps = seq_len // tm
    return pl.pallas_call(
        functools.partial(_attn_kernel, heads=MEM_HEADS),
        grid=(n // tm,),
        in_specs=[
            pl.BlockSpec((tm, d), lambda i: (i, 0)),
            _const_spec((1, d)),
            _const_spec(w_q.shape),
            pl.BlockSpec((n_mem, d), lambda i: (i // bps, 0)),
            pl.BlockSpec((n_mem, d), lambda i: (i // bps, 1)),
            _const_spec(w_o.shape),
        ],
        out_specs=pl.BlockSpec((tm, d), lambda i: (i, 0)),
        out_shape=jax.ShapeDtypeStruct((n, d), F32),
        compiler_params=_params(("arbitrary",)),
        name="mem_attn",
    )(h, g, w_q, kv, kv, w_o)


def _top_values(s, k):
    vals = []
    cur = s
    for _ in range(k):
        m = jnp.max(cur, axis=0, keepdims=True)
        vals.append(m)
        cur = jnp.where(cur == m, -jnp.inf, cur)
    return vals


def _candidate_pairs(k):
    return [(i, j) for i in range(k) for j in range(k) if (i + 1) * (j + 1) <= k]


def _route_kernel(h_ref, g_ref, wqt_ref, k1_ref, k2_ref, xt_ref, s1_ref, s2_ref, st_ref, *, topk):
    heads, nkeys, half = k1_ref.shape
    tm = h_ref.shape[0]
    xt = _rms(h_ref[...], g_ref[...]).T.astype(BF16)
    xt_ref[...] = xt
    qt = _dot(wqt_ref[...], xt)
    pairs = _candidate_pairs(topk)
    pad = (-len(pairs)) % 8
    theta, a1, b1, rz = [], [], [], []
    for hh in range(heads):
        base = hh * 2 * half
        s1 = _dot(k1_ref[hh], qt[base:base + half].astype(BF16))
        s2 = _dot(k2_ref[hh], qt[base + half:base + 2 * half].astype(BF16))
        s1_ref[hh] = s1
        s2_ref[hh] = s2
        a = _top_values(s1, topk)
        b = _top_values(s2, topk)
        cand = jnp.concatenate([a[i] + b[j] for i, j in pairs] + [jnp.full((pad, tm), -jnp.inf, F32)], axis=0)
        c = _top_values(cand, topk)
        z = c[0] * 0.0
        for ck in c:
            z = z + jnp.exp(ck - c[0])
        theta.append(c[topk - 1])
        a1.append(a[0])
        b1.append(b[0])
        rz.append(1.0 / z)
    st_ref[...] = jnp.concatenate(theta + a1 + b1 + rz, axis=0)


def _peer_route(h, g, wqt, k1, k2, *, tm):
    n, d = h.shape
    heads, nkeys, _ = k1.shape
    return pl.pallas_call(
        functools.partial(_route_kernel, topk=PEER_TOPK),
        grid=(n // tm,),
        in_specs=[pl.BlockSpec((tm, d), lambda i: (i, 0)), _const_spec((1, d)), _const_spec(wqt.shape),
                  _const_spec(k1.shape), _const_spec(k2.shape)],
        out_specs=[
            pl.BlockSpec((d, tm), lambda i: (0, i)),
            pl.BlockSpec((heads, nkeys, tm), lambda i: (0, 0, i)),
            pl.BlockSpec((heads, nkeys, tm), lambda i: (0, 0, i)),
            pl.BlockSpec((4 * heads, tm), lambda i: (0, i)),
        ],
        out_shape=[
            jax.ShapeDtypeStruct((d, n), BF16),
            jax.ShapeDtypeStruct((heads, nkeys, n), F32),
            jax.ShapeDtypeStruct((heads, nkeys, n), F32),
            jax.ShapeDtypeStruct((4 * heads, n), F32),
        ],
        compiler_params=_params(("arbitrary",)),
        name="peer_route",
    )(h, g, wqt, k1, k2)


def _peer_kernel(xt_ref, u_ref, v_ref, s1_ref, s2_ref, st_ref, h_ref, fg_ref, o_ref,
                 e1_ref, e2_ref, act_ref, w_ref):
    j = pl.program_id(1)
    heads, nkeys, tm = s2_ref.shape
    n1 = s1_ref.shape[1]

    @pl.when(j == 0)
    def _():
        o_ref[...] = jnp.zeros(o_ref.shape, F32)
        for hh in range(heads):
            e2_ref[hh] = (jnp.exp(s2_ref[hh] - st_ref[2 * heads + hh:2 * heads + hh + 1, :])
                          * st_ref[3 * heads + hh:3 * heads + hh + 1, :])

    for hh in range(heads):
        e1_ref[hh] = jnp.exp(s1_ref[hh] - st_ref[heads + hh:heads + hh + 1, :])
    act_ref[...] = _dot(u_ref[...], xt_ref[...])
    for il in range(n1):
        rows = slice(il * nkeys, (il + 1) * nkeys)
        for lt in range(tm // LANES):
            ls = slice(lt * LANES, (lt + 1) * LANES)
            gsum = jnp.zeros((nkeys, LANES), F32)
            for hh in range(heads):
                sm = s1_ref[hh, il:il + 1, ls] + s2_ref[hh, :, ls]
                pe = e1_ref[hh, il:il + 1, ls] * e2_ref[hh, :, ls]
                gsum = gsum + jnp.where(sm >= st_ref[hh:hh + 1, ls], pe, 0.0)
            w_ref[rows, ls] = (gsum * jax.nn.gelu(act_ref[rows, ls])).astype(BF16)
    o_ref[...] += lax.dot_general(w_ref[...], v_ref[...], _TN, preferred_element_type=F32)

    @pl.when(j == pl.num_programs(1) - 1)
    def _():
        o_ref[...] = _rms(h_ref[...] + o_ref[...], fg_ref[...])


def _peer_main(xt, u_tab, v_tab, s1t, s2t, st, h, fg, *, tm, te):
    d, n = xt.shape
    heads, nkeys, _ = s1t.shape
    ne = u_tab.shape[0]
    n1 = te // nkeys
    assert te % nkeys == 0 and n1 % 8 == 0 and ne % te == 0 and n % tm == 0 and tm % LANES == 0
    return pl.pallas_call(
        _peer_kernel,
        grid=(n // tm, ne // te),
        in_specs=[
            pl.BlockSpec((d, tm), lambda i, j: (0, i)),
            pl.BlockSpec((te, d), lambda i, j: (j, 0)),
            pl.BlockSpec((te, d), lambda i, j: (j, 0)),
            pl.BlockSpec((heads, n1, tm), lambda i, j: (0, j, i)),
            pl.BlockSpec((heads, nkeys, tm), lambda i, j: (0, 0, i)),
            pl.BlockSpec((4 * heads, tm), lambda i, j: (0, i)),
            pl.BlockSpec((tm, d), lambda i, j: (i, 0)),
            _const_spec((1, d)),
        ],
        out_specs=pl.BlockSpec((tm, d), lambda i, j: (i, 0)),
        out_shape=jax.ShapeDtypeStruct((n, d), F32),
        scratch_shapes=[
            pltpu.VMEM((heads, n1, tm), F32),
            pltpu.VMEM((heads, nkeys, tm), F32),
            pltpu.VMEM((te, tm), F32),
            pltpu.VMEM((te, tm), BF16),
        ],
        compiler_params=_params(("arbitrary", "arbitrary")),
        name="peer_main",
    )(xt, u_tab, v_tab, s1t, s2t, st, h, fg)


def kernel(x, mem, norm_mix_g, w_in, conv_w, conv_b, conv_ln_g, conv_ln_b, ssm_lam_re, ssm_lam_im, ssm_log_dt, ssm_b_re, ssm_b_im, ssm_c_re, ssm_c_im, ssm_d, ssm_glu_w, grp_norm_conv_g, grp_norm_ssm_g, w_out, norm_mem_g, mem_norm_g, w_q_mem, w_kv_mem, w_o_mem, norm_ffn_g, peer_w_q, peer_keys, peer_u, peer_v, final_norm_g):
    bsz, seq, dm = x.shape
    n = bsz * seq
    n_mem = mem.shape[1]
    depth = w_in.shape[0]
    grp, pst = ssm_lam_re.shape[1], ssm_lam_re.shape[2]
    hch = ssm_b_re.shape[-1]
    t_chunk = min(S5_CHUNK, seq)
    chunks = seq // t_chunk
    tm = min(512, seq)
    row = lambda a: a.reshape(1, -1).astype(F32)

    h = x.reshape(n, dm)
    mem2 = mem.reshape(bsz * n_mem, dm)
    out = None
    for i in range(depth):
        yc, u = _mix_in(h, row(norm_mix_g[i]), w_in[i].astype(BF16), conv_w[i], row(conv_b[i]),
                        row(conv_ln_g[i]), row(conv_ln_b[i]), row(grp_norm_conv_g[i]), seq_len=seq, tm=tm)
        uc = (u.reshape(bsz * chunks, t_chunk, grp, hch).transpose(2, 0, 1, 3)
              .reshape(grp, bsz * chunks, t_chunk * hch).astype(BF16))
        ysg = _s5_core(uc, ssm_lam_re[i].reshape(grp, 1, pst), ssm_lam_im[i].reshape(grp, 1, pst),
                       ssm_log_dt[i].reshape(grp, 1, 1),
                       ssm_b_re[i].transpose(0, 2, 1), ssm_b_im[i].transpose(0, 2, 1),
                       ssm_c_re[i], ssm_c_im[i], t_chunk=t_chunk, chunks_per_seq=chunks, nseq=bsz)
        ys = (ysg.reshape(grp, bsz * chunks, t_chunk, hch).transpose(1, 2, 0, 3).reshape(n, grp * hch))
        h = _mix_out(ys, u, row(ssm_d[i]), ssm_glu_w[i].astype(BF16), row(grp_norm_ssm_g[i]), yc,
                     w_out[i].astype(BF16), h, tm=tm)
        kv = _kv_proj(mem2, row(mem_norm_g[i]), w_kv_mem[i].astype(BF16), tn=min(1024, 2 * dm))
        h = _mem_attn(h, row(norm_mem_g[i]), w_q_mem[i].astype(BF16), kv, w_o_mem[i].astype(BF16),
                      seq_len=seq, n_mem=n_mem, tm=tm)
        keys = peer_keys[i].astype(BF16)
        xt, s1t, s2t, st = _peer_route(h, row(norm_ffn_g[i]), peer_w_q[i].T.astype(BF16),
                                       keys[:, 0], keys[:, 1], tm=min(256, seq))
        last = i == depth - 1
        fg = row(final_norm_g) if last else None
        assert last, "final norm is fused into the last layer's PEER kernel"
        h = _peer_main(xt, peer_u[i].astype(BF16), peer_v[i].astype(BF16), s1t, s2t, st, h, fg,
                       tm=tm, te=min(1024, peer_u.shape[1]))
        out = h
    return out.reshape(bsz, seq, dm)
```

```python
import functools
import math

import jax
import jax.numpy as jnp
from jax import lax
from jax.experimental import pallas as pl
from jax.experimental.pallas import tpu as pltpu

F32 = jnp.float32
BF16 = jnp.bfloat16
EPS = 1e-6
MEM_HEADS = 4
PEER_TOPK = 16
LANES = 128
CONV_HALO = 32
S5_CHUNK = 64
VMEM_LIMIT = 56 * 1024 * 1024

_NT = (((1,), (1,)), ((), ()))
_TN = (((0,), (0,)), ((), ()))


def _rms(x, g):
    return x * lax.rsqrt(jnp.mean(x * x, axis=-1, keepdims=True) + EPS) * g


def _dot(a, b):
    return jnp.dot(a, b, preferred_element_type=F32)


def _const_spec(shape):
    nd = len(shape)
    return pl.BlockSpec(shape, lambda *_: (0,) * nd, pipeline_mode=pl.Buffered(1))


def _params(sem, flags=None):
    return pltpu.CompilerParams(dimension_semantics=sem, vmem_limit_bytes=VMEM_LIMIT, flags=flags)


def _mix_in_kernel(x_ref, g_ref, w_ref, cw_ref, cb_ref, lng_ref, lnb_ref, gng_ref,
                   yc_ref, u_ref, xe_ref, *, blocks_per_seq, dc, kw, rb):
    i = pl.program_id(0)
    tm = x_ref.shape[0]
    hn = _rms(x_ref[...], g_ref[...]).astype(BF16)
    a = _dot(hn, w_ref[:, 0:dc])
    gate = _dot(hn, w_ref[:, dc:2 * dc])
    u_ref[...] = _dot(hn, w_ref[:, 2 * dc:])

    @pl.when(i % blocks_per_seq == 0)
    def _():
        xe_ref[0:CONV_HALO, :] = jnp.zeros((CONV_HALO, dc), F32)

    xe_ref[CONV_HALO:CONV_HALO + tm, :] = a * jax.nn.sigmoid(gate)

    for r0 in range(0, tm, rb):
        acc = jnp.broadcast_to(cb_ref[...], (rb, dc))
        for k in range(kw):
            start = r0 + CONV_HALO - (kw - 1) + k
            acc = acc + cw_ref[k:k + 1, :] * xe_ref[start:start + rb, :]
        mu = jnp.mean(acc, axis=-1, keepdims=True)
        xc = acc - mu
        y = xc * lax.rsqrt(jnp.mean(xc * xc, axis=-1, keepdims=True) + EPS) * lng_ref[...] + lnb_ref[...]
        y = y * jax.nn.sigmoid(y)
        yc_ref[r0:r0 + rb, :] = _rms(y, gng_ref[...]).astype(BF16)
    xe_ref[0:CONV_HALO, :] = xe_ref[tm:tm + CONV_HALO, :]


def _mix_in(x2, g, w_in, conv_w, conv_b, ln_g, ln_b, gn_g, *, seq_len, tm):
    n, d = x2.shape
    kw, dc = conv_w.shape
    ds = w_in.shape[1] - 2 * dc
    assert n % tm == 0 and seq_len % tm == 0 and kw - 1 <= CONV_HALO <= tm
    kern = functools.partial(_mix_in_kernel, blocks_per_seq=seq_len // tm, dc=dc, kw=kw, rb=32)
    return pl.pallas_call(
        kern,
        grid=(n // tm,),
        in_specs=[
            pl.BlockSpec((tm, d), lambda i: (i, 0)),
            _const_spec((1, d)),
            _const_spec(w_in.shape),
            _const_spec((kw, dc)),
            _const_spec((1, dc)),
            _const_spec((1, dc)),
            _const_spec((1, dc)),
            _const_spec((1, dc)),
        ],
        out_specs=[
            pl.BlockSpec((tm, dc), lambda i: (i, 0)),
            pl.BlockSpec((tm, ds), lambda i: (i, 0)),
        ],
        out_shape=[jax.ShapeDtypeStruct((n, dc), BF16), jax.ShapeDtypeStruct((n, ds), F32)],
        scratch_shapes=[pltpu.VMEM((tm + CONV_HALO, dc), F32)],
        compiler_params=_params(("arbitrary",)),
        name="mix_in",
    )(x2, g, w_in, conv_w, conv_b, ln_g, ln_b, gn_g)


def _rep_rows(a, reps):
    t, p = a.shape
    return jnp.concatenate([jnp.broadcast_to(a[k:k + 1, :], (reps, p)) for k in range(t)], axis=0)


def _tile_rows(a, reps):
    h, p = a.shape
    return jnp.broadcast_to(a[None], (reps, h, p)).reshape(reps * h, p)


def _s5_kernel(uc_ref, lre_ref, lim_ref, ldt_ref, btr_ref, bti_ref, cr_ref, ci_ref, y_ref,
               m_ref, slr_ref, sli_ref, spr_ref, spi_ref, *, t_chunk, chunks_per_seq, nseq):
    h, p = cr_ref.shape[1], cr_ref.shape[2]
    w = t_chunk * h
    lre, lim = lre_ref[0], lim_ref[0]
    dt = jnp.exp(ldt_ref[0])
    are, aim = lre * dt, lim * dt
    mag = jnp.exp(are)
    lbr, lbi = mag * jnp.cos(aim), mag * jnp.sin(aim)
    den = lre * lre + lim * lim
    kr = ((lbr - 1.0) * lre + lbi * lim) / den
    ki = (lbi * lre - (lbr - 1.0) * lim) / den
    btr, bti = btr_ref[0], bti_ref[0]
    bbr, bbi = kr * btr - ki * bti, kr * bti + ki * btr
    cr, ci = cr_ref[0], ci_ref[0]

    tau = lax.broadcasted_iota(jnp.int32, (t_chunk, 1), 0).astype(F32)

    def lam_pow(e):
        m = jnp.exp(e * are)
        return m * jnp.cos(e * aim), m * jnp.sin(e * aim)

    l0r, l0i = lam_pow(tau)
    lrr, lri = lam_pow((t_chunk - 1.0) - tau)
    ltr, lti = lam_pow(jnp.full((1, 1), float(t_chunk), F32))

    l0r_rep, l0i_rep = _rep_rows(l0r, h), _rep_rows(l0i, h)
    c_r, c_i = _tile_rows(cr, t_chunk), _tile_rows(ci, t_chunk)
    lc0r = l0r_rep * c_r - l0i_rep * c_i
    lc0i = l0r_rep * c_i + l0i_rep * c_r
    lc1r = lc0r * lbr - lc0i * lbi
    lc1i = lc0r * lbi + lc0i * lbr

    r0 = (lax.dot_general(bbr, lc0r, _NT, preferred_element_type=F32, precision=lax.Precision.HIGHEST)
          - lax.dot_general(bbi, lc0i, _NT, preferred_element_type=F32, precision=lax.Precision.HIGHEST))
    ext = jnp.concatenate([jnp.zeros((h, w), F32), r0], axis=1)
    per_tile = LANES // h
    for b in range(per_tile):
        sb = ext[:, LANES - h * b: LANES - h * b + 2 * w - LANES]
        for a in range(w // LANES):
            i = per_tile * a + b
            m_ref[h * i:h * (i + 1), :] = sb[:, w - LANES * (a + 1): 2 * w - LANES * (a + 1)].astype(BF16)

    lrr_rep, lri_rep = _rep_rows(lrr, h), _rep_rows(lri, h)
    b_r, b_i = _tile_rows(bbr, t_chunk), _tile_rows(bbi, t_chunk)
    wsr = (lrr_rep * b_r - lri_rep * b_i).astype(BF16)
    wsi = (lrr_rep * b_i + lri_rep * b_r).astype(BF16)
    uc = uc_ref[0]
    slr_ref[...] = _dot(uc, wsr)
    sli_ref[...] = _dot(uc, wsi)

    def step(c, carry):
        new = []
        for s in range(nseq):
            sr, si = carry[2 * s], carry[2 * s + 1]
            r = s * chunks_per_seq + c
            spr_ref[pl.ds(r, 1), :] = sr
            spi_ref[pl.ds(r, 1), :] = si
            new.append(ltr * sr - lti * si + slr_ref[pl.ds(r, 1), :])
            new.append(ltr * si + lti * sr + sli_ref[pl.ds(r, 1), :])
        return tuple(new)

    lax.fori_loop(0, chunks_per_seq, step, tuple(jnp.zeros((1, p), F32) for _ in range(2 * nseq)))

    y = _dot(uc, m_ref[...])
    y = y + lax.dot_general(spr_ref[...].astype(BF16), lc1r.astype(BF16), _NT, preferred_element_type=F32)
    y = y - lax.dot_general(spi_ref[...].astype(BF16), lc1i.astype(BF16), _NT, preferred_element_type=F32)
    y_ref[0] = y


def _s5_core(uc, lam_re, lam_im, log_dt, bt_re, bt_im, c_re, c_im, *, t_chunk, chunks_per_seq, nseq):
    g, r, w = uc.shape
    h, p = c_re.shape[1], c_re.shape[2]
    assert w == t_chunk * h and LANES % h == 0 and w % LANES == 0 and r == nseq * chunks_per_seq
    kern = functools.partial(_s5_kernel, t_chunk=t_chunk, chunks_per_seq=chunks_per_seq, nseq=nseq)
    grp = lambda *shape: pl.BlockSpec((1,) + shape, lambda i: (i,) + (0,) * len(shape))
    return pl.pallas_call(
        kern,
        grid=(g,),
        in_specs=[grp(r, w), grp(1, p), grp(1, p), grp(1, 1), grp(h, p), grp(h, p), grp(h, p), grp(h, p)],
        out_specs=grp(r, w),
        out_shape=jax.ShapeDtypeStruct((g, r, w), F32),
        scratch_shapes=[pltpu.VMEM((w, w), BF16)] + [pltpu.VMEM((r, p), F32)] * 4,
        compiler_params=_params(("arbitrary",)),
        name="s5_core",
    )(uc, lam_re, lam_im, log_dt, bt_re, bt_im, c_re, c_im)


def _mix_out_kernel(ys_ref, u_ref, d_ref, gluw_ref, gsg_ref, yc_ref, wout_ref, x_ref, h_ref, *, dc):
    y = ys_ref[...] + d_ref[...] * u_ref[...]
    z = jax.nn.gelu(y)
    zz = z * jax.nn.sigmoid(_dot(z.astype(BF16), gluw_ref[...]))
    ysn = _rms(zz, gsg_ref[...]).astype(BF16)
    h_ref[...] = x_ref[...] + _dot(yc_ref[...], wout_ref[0:dc, :]) + _dot(ysn, wout_ref[dc:, :])


def _mix_out(ys, u, d, glu_w, gs_g, yc, w_out, x2, *, tm):
    n, dm = x2.shape
    dc, ds = yc.shape[1], ys.shape[1]
    tok = lambda c: pl.BlockSpec((tm, c), lambda i: (i, 0))
    return pl.pallas_call(
        functools.partial(_mix_out_kernel, dc=dc),
        grid=(n // tm,),
        in_specs=[tok(ds), tok(ds), _const_spec((1, ds)), _const_spec(glu_w.shape), _const_spec((1, ds)),
                  tok(dc), _const_spec(w_out.shape), tok(dm)],
        out_specs=tok(dm),
        out_shape=jax.ShapeDtypeStruct((n, dm), F32),
        compiler_params=_params(("arbitrary",)),
        name="mix_out",
    )(ys, u, d, glu_w, gs_g, yc, w_out, x2)


def _kv_kernel(mem_ref, g_ref, w_ref, kv_ref):
    kv_ref[...] = _dot(_rms(mem_ref[...], g_ref[...]).astype(BF16), w_ref[...]).astype(BF16)


def _kv_proj(mem2, g, w_kv, *, tn):
    nm, d = mem2.shape
    return pl.pallas_call(
        _kv_kernel,
        grid=(w_kv.shape[1] // tn,),
        in_specs=[_const_spec((nm, d)), _const_spec((1, d)), pl.BlockSpec((d, tn), lambda j: (0, j))],
        out_specs=pl.BlockSpec((nm, tn), lambda j: (0, j)),
        out_shape=jax.ShapeDtypeStruct((nm, w_kv.shape[1]), BF16),
        compiler_params=_params(("arbitrary",)),
        name="kv_proj",
    )(mem2, g, w_kv)


def _attn_kernel(h_ref, g_ref, wq_ref, k_ref, v_ref, wo_ref, o_ref, *, heads):
    hres = h_ref[...]
    d = hres.shape[1]
    hd = d // heads
    q = _dot(_rms(hres, g_ref[...]).astype(BF16), wq_ref[...])
    scale = 1.0 / math.sqrt(hd)
    outs = []
    for hh in range(heads):
        cs = slice(hh * hd, (hh + 1) * hd)
        s = lax.dot_general(q[:, cs].astype(BF16), k_ref[:, cs], _NT, preferred_element_type=F32) * scale
        e = jnp.exp(s - jnp.max(s, axis=-1, keepdims=True))
        pr = e / jnp.sum(e, axis=-1, keepdims=True)
        outs.append(_dot(pr.astype(BF16), v_ref[:, cs]).astype(BF16))
    o_ref[...] = hres + _dot(jnp.concatenate(outs, axis=1), wo_ref[...])


def _mem_attn(h, g, w_q, kv, w_o, *, seq_len, n_mem, tm):
    n, d = h.shape
    bps = seq_len // tm
    return pl.pallas_call(
        functools.partial(_attn_kernel, heads=MEM_HEADS),
        grid=(n // tm,),
        in_specs=[
            pl.BlockSpec((tm, d), lambda i: (i, 0)),
            _const_spec((1, d)),
            _const_spec(w_q.shape),
            pl.BlockSpec((n_mem, d), lambda i: (i // bps, 0)),
            pl.BlockSpec((n_mem, d), lambda i: (i // bps, 1)),
            _const_spec(w_o.shape),
        ],
        out_specs=pl.BlockSpec((tm, d), lambda i: (i, 0)),
        out_shape=jax.ShapeDtypeStruct((n, d), F32),
        compiler_params=_params(("arbitrary",)),
        name="mem_attn",
    )(h, g, w_q, kv, kv, w_o)


LOG2E = 1.4426950408889634


def _top_values(s, k):
    vals = []
    cur = s
    for _ in range(k):
        m = jnp.max(cur, axis=0, keepdims=True)
        vals.append(m)
        cur = jnp.where(cur == m, -jnp.inf, cur)
    return vals


def _candidate_pairs(k):
    return [(i, j) for i in range(k) for j in range(k) if (i + 1) * (j + 1) <= k]


def _route_kernel(h_ref, g_ref, wqt_ref, k1_ref, k2_ref, xt_ref, r1_ref, s2l_ref, thr_ref, *, topk):
    heads, nkeys, half = k1_ref.shape
    tm = h_ref.shape[0]
    xt = _rms(h_ref[...], g_ref[...]).T.astype(BF16)
    xt_ref[...] = xt
    qt = _dot(wqt_ref[...], xt)
    pairs = _candidate_pairs(topk + 1)
    pad = (-len(pairs)) % 8
    thr = []
    for hh in range(heads):
        base = hh * 2 * half
        s1 = _dot(k1_ref[hh], qt[base:base + half].astype(BF16))
        s2 = _dot(k2_ref[hh], qt[base + half:base + 2 * half].astype(BF16))
        a = _top_values(s1, topk + 1)
        b = _top_values(s2, topk + 1)
        cand = jnp.concatenate([a[i] + b[j] for i, j in pairs] + [jnp.full((pad, tm), -jnp.inf, F32)], axis=0)
        c = _top_values(cand, topk + 1)
        z = jnp.zeros_like(c[0])
        for ck in c[:topk]:
            z = z + jnp.exp(ck - c[0])
        shift = c[0] + jnp.log(z)
        theta = 0.5 * (c[topk - 1] + c[topk])
        r1_ref[hh] = (s1 - shift) * LOG2E - 1.0
        s2l_ref[hh] = s2 * LOG2E
        thr.append((theta - shift) * LOG2E - 1.0)
    thr_ref[...] = jnp.concatenate(thr, axis=0)


def _peer_route(h, g, wqt, k1, k2, *, tm):
    n, d = h.shape
    heads, nkeys, _ = k1.shape
    return pl.pallas_call(
        functools.partial(_route_kernel, topk=PEER_TOPK),
        grid=(n // tm,),
        in_specs=[pl.BlockSpec((tm, d), lambda i: (i, 0)), _const_spec((1, d)), _const_spec(wqt.shape),
                  _const_spec(k1.shape), _const_spec(k2.shape)],
        out_specs=[
            pl.BlockSpec((d, tm), lambda i: (0, i)),
            pl.BlockSpec((heads, nkeys, tm), lambda i: (0, 0, i)),
            pl.BlockSpec((heads, nkeys, tm), lambda i: (0, 0, i)),
            pl.BlockSpec((heads, tm), lambda i: (0, i)),
        ],
        out_shape=[
            jax.ShapeDtypeStruct((d, n), BF16),
            jax.ShapeDtypeStruct((heads, nkeys, n), F32),
            jax.ShapeDtypeStruct((heads, nkeys, n), F32),
            jax.ShapeDtypeStruct((heads, n), F32),
        ],
        compiler_params=_params(("arbitrary",)),
        name="peer_route",
    )(h, g, wqt, k1, k2)


GELU_C0 = math.sqrt(2.0 / math.pi)
GELU_C1 = GELU_C0 * 0.044715
PEER_ROWS = 64


def _peer_stage(xt_ref, u_ref, v_ref, r1_ref, s2l_ref, thr_ref, o_ref, act_w_ref, act_r_ref, w_ref):
    heads, nkeys, tm = s2l_ref.shape
    n1 = r1_ref.shape[1]
    for il in range(n1):
        for lt in range(tm // LANES):
            ls = slice(lt * LANES, (lt + 1) * LANES)
            for kb in range(0, nkeys, PEER_ROWS):
                ks = slice(kb, kb + PEER_ROWS)
                rows = slice(il * nkeys + kb, il * nkeys + kb + PEER_ROWS)
                gsum = jnp.zeros((PEER_ROWS, LANES), F32)
                for hh in range(heads):
                    v = s2l_ref[hh, ks, ls] + r1_ref[hh, il:il + 1, ls]
                    gsum = gsum + jnp.where(v >= thr_ref[hh:hh + 1, ls], jnp.exp2(v), 0.0)
                a = act_r_ref[rows, ls]
                t = jnp.tanh(a * (GELU_C0 + GELU_C1 * (a * a)))
                w_ref[rows, ls] = ((gsum * a) * (1.0 + t)).astype(BF16)
    act_w_ref[...] = _dot(u_ref[...], xt_ref[...])
    o_ref[...] += lax.dot_general(w_ref[...], v_ref[...], _TN, preferred_element_type=F32)


def _peer_kernel(xt_ref, u_ref, v_ref, r1_ref, s2l_ref, thr_ref, h_ref, fg_ref, o_ref,
                 act0_ref, act1_ref, w_ref, *, nj):
    s = pl.program_id(0)
    jp = jnp.maximum(s - 1, 0) % nj

    @pl.when(s == 0)
    def _():
        act1_ref[...] = jnp.zeros(act1_ref.shape, F32)

    @pl.when(jp == 0)
    def _():
        o_ref[...] = jnp.zeros(o_ref.shape, F32)

    args = (xt_ref, u_ref, v_ref, r1_ref, s2l_ref, thr_ref, o_ref)

    @pl.when(s % 2 == 0)
    def _():
        _peer_stage(*args, act0_ref, act1_ref, w_ref)

    @pl.when(s % 2 == 1)
    def _():
        _peer_stage(*args, act1_ref, act0_ref, w_ref)

    @pl.when(jnp.logical_and(jp == nj - 1, s > 0))
    def _():
        o_ref[...] = _rms(h_ref[...] + o_ref[...], fg_ref[...])


def _peer_main(xt, u_tab, v_tab, r1, s2l, thr, h, fg, *, tm, te):
    d, n = xt.shape
    heads, nkeys, _ = s2l.shape
    ne = u_tab.shape[0]
    n1 = te // nkeys
    assert te % nkeys == 0 and n1 % 8 == 0 and ne % te == 0 and n % tm == 0 and tm % LANES == 0
    assert nkeys % PEER_ROWS == 0
    ni, nj = n // tm, ne // te
    steps = ni * nj + 1
    cur = lambda s: jnp.minimum(s, steps - 2)
    prev = lambda s: jnp.maximum(s - 1, 0)
    return pl.pallas_call(
        functools.partial(_peer_kernel, nj=nj),
        grid=(steps,),
        in_specs=[
            pl.BlockSpec((d, tm), lambda s: (0, cur(s) // nj)),
            pl.BlockSpec((te, d), lambda s: (cur(s) % nj, 0)),
            pl.BlockSpec((te, d), lambda s: (prev(s) % nj, 0)),
            pl.BlockSpec((heads, n1, tm), lambda s: (0, prev(s) % nj, prev(s) // nj)),
            pl.BlockSpec((heads, nkeys, tm), lambda s: (0, 0, prev(s) // nj)),
            pl.BlockSpec((heads, tm), lambda s: (0, prev(s) // nj)),
            pl.BlockSpec((tm, d), lambda s: (prev(s) // nj, 0)),
            _const_spec((1, d)),
        ],
        out_specs=pl.BlockSpec((tm, d), lambda s: (prev(s) // nj, 0)),
        out_shape=jax.ShapeDtypeStruct((n, d), F32),
        scratch_shapes=[
            pltpu.VMEM((te, tm), F32),
            pltpu.VMEM((te, tm), F32),
            pltpu.VMEM((te, tm), BF16),
        ],
        compiler_params=_params(("arbitrary",)),
        name="peer_main",
    )(xt, u_tab, v_tab, r1, s2l, thr, h, fg)


def kernel(x, mem, norm_mix_g, w_in, conv_w, conv_b, conv_ln_g, conv_ln_b, ssm_lam_re, ssm_lam_im, ssm_log_dt, ssm_b_re, ssm_b_im, ssm_c_re, ssm_c_im, ssm_d, ssm_glu_w, grp_norm_conv_g, grp_norm_ssm_g, w_out, norm_mem_g, mem_norm_g, w_q_mem, w_kv_mem, w_o_mem, norm_ffn_g, peer_w_q, peer_keys, peer_u, peer_v, final_norm_g):
    bsz, seq, dm = x.shape
    n = bsz * seq
    n_mem = mem.shape[1]
    depth = w_in.shape[0]
    grp, pst = ssm_lam_re.shape[1], ssm_lam_re.shape[2]
    hch = ssm_b_re.shape[-1]
    t_chunk = min(S5_CHUNK, seq)
    chunks = seq // t_chunk
    tm = min(512, seq)
    row = lambda a: a.reshape(1, -1).astype(F32)

    h = x.reshape(n, dm)
    mem2 = mem.reshape(bsz * n_mem, dm)
    out = None
    for i in range(depth):
        yc, u = _mix_in(h, row(norm_mix_g[i]), w_in[i].astype(BF16), conv_w[i], row(conv_b[i]),
                        row(conv_ln_g[i]), row(conv_ln_b[i]), row(grp_norm_conv_g[i]), seq_len=seq, tm=tm)
        uc = (u.reshape(bsz * chunks, t_chunk, grp, hch).transpose(2, 0, 1, 3)
              .reshape(grp, bsz * chunks, t_chunk * hch).astype(BF16))
        ysg = _s5_core(uc, ssm_lam_re[i].reshape(grp, 1, pst), ssm_lam_im[i].reshape(grp, 1, pst),
                       ssm_log_dt[i].reshape(grp, 1, 1),
                       ssm_b_re[i].transpose(0, 2, 1), ssm_b_im[i].transpose(0, 2, 1),
                       ssm_c_re[i], ssm_c_im[i], t_chunk=t_chunk, chunks_per_seq=chunks, nseq=bsz)
        ys = (ysg.reshape(grp, bsz * chunks, t_chunk, hch).transpose(1, 2, 0, 3).reshape(n, grp * hch))
        h = _mix_out(ys, u, row(ssm_d[i]), ssm_glu_w[i].astype(BF16), row(grp_norm_ssm_g[i]), yc,
                     w_out[i].astype(BF16), h, tm=tm)
        kv = _kv_proj(mem2, row(mem_norm_g[i]), w_kv_mem[i].astype(BF16), tn=min(1024, 2 * dm))
        h = _mem_attn(h, row(norm_mem_g[i]), w_q_mem[i].astype(BF16), kv, w_o_mem[i].astype(BF16),
                      seq_len=seq, n_mem=n_mem, tm=tm)
        keys = peer_keys[i].astype(BF16)
        xt, r1, s2l, thr = _peer_route(h, row(norm_ffn_g[i]), peer_w_q[i].T.astype(BF16),
                                       keys[:, 0], keys[:, 1], tm=min(256, seq))
        last = i == depth - 1
        fg = row(final_norm_g) if last else None
        assert last, "final norm is fused into the last layer's PEER kernel"
        h = _peer_main(xt, peer_u[i].astype(BF16), peer_v[i].astype(BF16), r1, s2l, thr, h, fg,
                       tm=tm, te=min(1024, peer_u.shape[1]))
        out = h
    return out.reshape(bsz, seq, dm)
```
